```python
import jax, jax.numpy as jnp
from jax import lax
import numpy as np

D_MODEL = 2048
BATCH = 4
SEQ = 2048
DEPTH = 2
DEC_BATCH = 128
DEC_SEQ = 4
PAST_LEN = 16384
PAGE_SIZE = 128

N_EVEN = (DEPTH + 1) // 2
N_ODD = DEPTH // 2
D_A = D_MODEL // 2
CONV_A = 3
D_B = D_MODEL // 2
CONV_B = 31
D_IN_AB = 3 * D_A + 2 * D_B
D_C = D_MODEL
H_C = 8
CHUNK = 128
N_EXPERTS = 64
TOP_K = 8
N_GROUPS = 8
TOPK_GROUPS = 4
D_EXPERT = D_MODEL // 4
D_SHARED = D_MODEL // 4
ROUTED_SCALE = 2.5
ALPHA = (2 * DEPTH) ** 0.25
BETA = (8 * DEPTH) ** -0.25
LN_EPS = 1e-5

kernel_name = 'hybrid_conv_gmlp_moe_deepnorm_step'


def layer_norm(x, g, b):
    xf = x.astype(jnp.float32)
    mu = jnp.mean(xf, axis=-1, keepdims=True)
    var = jnp.mean(jnp.square(xf - mu), axis=-1, keepdims=True)
    y = (xf - mu) * lax.rsqrt(var + LN_EPS) * g.astype(jnp.float32) + b.astype(jnp.float32)
    return y.astype(x.dtype)


def causal_dwconv(x, hist, w):
    full = jnp.concatenate([hist.astype(x.dtype), x], axis=1)
    out = lax.conv_general_dilated(full, w[:, None, :].astype(x.dtype), window_strides=(1,), padding='VALID',
                                   dimension_numbers=('NWC', 'WIO', 'NWC'), feature_group_count=x.shape[-1])
    return out, full[:, full.shape[1] - (w.shape[0] - 1):]


def chunk_mix(v, w_sp, b_sp):
    bsz, t, _ = v.shape
    n_chunks = -(-t // CHUNK)
    pad = n_chunks * CHUNK - t
    vp = jnp.pad(v, ((0, 0), (0, pad), (0, 0))).reshape(bsz, n_chunks, CHUNK, H_C, D_C // H_C)
    w = w_sp * jnp.tril(jnp.ones((CHUNK, CHUNK), w_sp.dtype))
    out = jnp.einsum('hij,bnjhc->bnihc', w, vp) + jnp.transpose(b_sp)[:, :, None]
    return out.reshape(bsz, n_chunks * CHUNK, D_C)[:, :t]


def ada_mod(c, w, b):
    m = jax.nn.silu(c) @ w + b
    return jnp.split(m[:, None, :], 6, axis=-1)


def mixer_ab(h, hist_a, hist_b, w_in, b_in, w_conv_a, w_conv_b, b_conv_b, ln_g, ln_b, w_out, b_out):
    z = h @ w_in + b_in
    a_h, a_b, a_c, b_lin, b_gate = jnp.split(z, [D_A, 2 * D_A, 3 * D_A, 3 * D_A + D_B], axis=-1)
    ya, new_a = causal_dwconv(a_c * a_h, hist_a, w_conv_a)
    ya = a_b * ya
    g = b_lin * jax.nn.sigmoid(b_gate)
    yb, new_b = causal_dwconv(g, hist_b, w_conv_b)
    yb = jax.nn.silu(layer_norm(yb + b_conv_b, ln_g, ln_b))
    out = jnp.concatenate([ya, yb], axis=-1) @ w_out + b_out
    return out, new_a, new_b


def mixer_c(h, w_in, b_in, ln_g, ln_b, w_sp, b_sp, w_out, b_out):
    z = jax.nn.gelu(h @ w_in + b_in)
    u, v = jnp.split(z, 2, axis=-1)
    v = layer_norm(v, ln_g, ln_b)
    y = u * chunk_mix(v, w_sp, b_sp)
    t = h.shape[1]
    start = ((t - 1) // CHUNK) * CHUNK
    return y @ w_out + b_out, v[:, start:]


def moe(h, w_router, b_router, w_gate_e, w_up_e, w_down_e, w_gate_s, w_up_s, w_down_s):
    bsz, t, d = h.shape
    xt = h.reshape(bsz * t, d)
    scores = jax.nn.sigmoid(jnp.dot(xt.astype(jnp.float32), w_router.astype(jnp.float32)))
    choice = scores + b_router.astype(jnp.float32)
    grp = choice.reshape(-1, N_GROUPS, N_EXPERTS // N_GROUPS)
    grp_score = jnp.sum(lax.top_k(grp, 2)[0], axis=-1)
    _, grp_idx = lax.top_k(grp_score, TOPK_GROUPS)
    grp_mask = jnp.sum(jax.nn.one_hot(grp_idx, N_GROUPS, dtype=jnp.float32), axis=1)
    exp_mask = jnp.repeat(grp_mask, N_EXPERTS // N_GROUPS, axis=1) > 0
    _, idx = lax.top_k(jnp.where(exp_mask, choice, -jnp.inf), TOP_K)
    wts = jnp.take_along_axis(scores, idx, axis=1)
    wts = wts / jnp.sum(wts, axis=-1, keepdims=True) * ROUTED_SCALE
    gates = jnp.einsum('tke,tk->te', jax.nn.one_hot(idx, N_EXPERTS, dtype=jnp.float32), wts).astype(h.dtype)

    def expert_step(acc, p):
        wg, wu, wd, g = p
        y = (jax.nn.silu(xt @ wg) * (xt @ wu)) @ wd
        return acc + g[:, None] * y, None

    routed, _ = lax.scan(expert_step, jnp.zeros_like(xt), (w_gate_e, w_up_e, w_down_e, jnp.transpose(gates)))
    shared = (jax.nn.silu(xt @ w_gate_s) * (xt @ w_up_s)) @ w_down_s
    return (routed + shared).reshape(bsz, t, d)


def trunk(x, c, hist_a, hist_b, p):
    new_a, new_b, new_v = [], [], []
    for l in range(DEPTH):
        sh_m, sc_m, g_m, sh_f, sc_f, g_f = ada_mod(c, p['w_ada'][l], p['b_ada'][l])
        h = x * (1 + sc_m) + sh_m
        i = l // 2
        if l % 2 == 0:
            out, sa, sb = mixer_ab(h, hist_a[i], hist_b[i], p['w_in_ab'][i], p['b_in_ab'][i], p['w_conv_a'][i],
                                   p['w_conv_b'][i], p['b_conv_b'][i], p['ln_cb_g'][i], p['ln_cb_b'][i],
                                   p['w_out_ab'][i], p['b_out_ab'][i])
            new_a.append(sa)
            new_b.append(sb)
        else:
            out, sv = mixer_c(h, p['w_in_c'][i], p['b_in_c'][i], p['ln_v_g'][i], p['ln_v_b'][i],
                              p['w_sp'][i], p['b_sp'][i], p['w_out_c'][i], p['b_out_c'][i])
            new_v.append(sv)
        x = layer_norm(ALPHA * x + (1 + g_m) * out, p['ln_g'][l, 0], p['ln_b'][l, 0])
        h = x * (1 + sc_f) + sh_f
        f = moe(h, p['w_router'][l], p['b_router'][l], p['w_gate_e'][l], p['w_up_e'][l], p['w_down_e'][l],
                p['w_gate_s'][l], p['w_up_s'][l], p['w_down_s'][l])
        x = layer_norm(ALPHA * x + (1 + g_f) * f, p['ln_g'][l, 1], p['ln_b'][l, 1])
    return x, jnp.stack(new_a), jnp.stack(new_b), jnp.stack(new_v)


def setup_inputs(seed: int = 0) -> dict:
    key = jax.random.key(seed)
    ks = iter(jax.random.split(key, 48))

    def nrm(shape, s):
        return jax.random.normal(next(ks), shape, jnp.float32) * s

    D = D_MODEL
    return {
        'x_prompt': nrm((BATCH, SEQ, D), 1.0),
        'x_sample': nrm((DEC_BATCH, DEC_SEQ, D), 1.0),
        'state_conv_a': nrm((N_EVEN, DEC_BATCH, CONV_A - 1, D_A), 1.0),
        'state_conv_b': nrm((N_EVEN, DEC_BATCH, CONV_B - 1, D_B), 0.5),
        'c_prompt': nrm((BATCH, D), 1.0),
        'c_sample': nrm((DEC_BATCH, D), 1.0),
        'w_ada': nrm((DEPTH, D, 6 * D), 0.1 * D ** -0.5),
        'b_ada': nrm((DEPTH, 6 * D), 0.02),
        'ln_g': 1.0 + nrm((DEPTH, 2, D), 0.02),
        'ln_b': nrm((DEPTH, 2, D), 0.02),
        'w_in_ab': nrm((N_EVEN, D, D_IN_AB), D ** -0.5),
        'b_in_ab': nrm((N_EVEN, D_IN_AB), 0.02),
        'w_conv_a': nrm((N_EVEN, CONV_A, D_A), CONV_A ** -0.5),
        'w_conv_b': nrm((N_EVEN, CONV_B, D_B), CONV_B ** -0.5),
        'b_conv_b': nrm((N_EVEN, D_B), 0.02),
        'ln_cb_g': 1.0 + nrm((N_EVEN, D_B), 0.02),
        'ln_cb_b': nrm((N_EVEN, D_B), 0.02),
        'w_out_ab': nrm((N_EVEN, D_A + D_B, D), BETA * (D_A + D_B) ** -0.5),
        'b_out_ab': nrm((N_EVEN, D), 0.02),
        'w_in_c': nrm((N_ODD, D, 2 * D_C), D ** -0.5),
        'b_in_c': nrm((N_ODD, 2 * D_C), 0.02),
        'ln_v_g': 1.0 + nrm((N_ODD, D_C), 0.02),
        'ln_v_b': nrm((N_ODD, D_C), 0.02),
        'w_sp': nrm((N_ODD, H_C, CHUNK, CHUNK), CHUNK ** -0.5),
        'b_sp': 1.0 + nrm((N_ODD, H_C, CHUNK), 0.1),
        'w_out_c': nrm((N_ODD, D_C, D), BETA * D_C ** -0.5),
        'b_out_c': nrm((N_ODD, D), 0.02),
        'w_router': nrm((DEPTH, D, N_EXPERTS), D ** -0.5),
        'b_router': nrm((DEPTH, N_EXPERTS), 0.01),
        'w_gate_e': nrm((DEPTH, N_EXPERTS, D, D_EXPERT), D ** -0.5),
        'w_up_e': nrm((DEPTH, N_EXPERTS, D, D_EXPERT), D ** -0.5),
        'w_down_e': nrm((DEPTH, N_EXPERTS, D_EXPERT, D), BETA * D_EXPERT ** -0.5),
        'w_gate_s': nrm((DEPTH, D, D_SHARED), D ** -0.5),
        'w_up_s': nrm((DEPTH, D, D_SHARED), D ** -0.5),
        'w_down_s': nrm((DEPTH, D_SHARED, D), BETA * D_SHARED ** -0.5),
    }


def reference(x_prompt, x_sample, state_conv_a, state_conv_b, c_prompt, c_sample,
              w_ada, b_ada, ln_g, ln_b,
              w_in_ab, b_in_ab, w_conv_a, w_conv_b, b_conv_b, ln_cb_g, ln_cb_b, w_out_ab, b_out_ab,
              w_in_c, b_in_c, ln_v_g, ln_v_b, w_sp, b_sp, w_out_c, b_out_c,
              w_router, b_router, w_gate_e, w_up_e, w_down_e, w_gate_s, w_up_s, w_down_s):
    p = dict(w_ada=w_ada, b_ada=b_ada, ln_g=ln_g, ln_b=ln_b,
             w_in_ab=w_in_ab, b_in_ab=b_in_ab, w_conv_a=w_conv_a, w_conv_b=w_conv_b, b_conv_b=b_conv_b,
             ln_cb_g=ln_cb_g, ln_cb_b=ln_cb_b, w_out_ab=w_out_ab, b_out_ab=b_out_ab,
             w_in_c=w_in_c, b_in_c=b_in_c, ln_v_g=ln_v_g, ln_v_b=ln_v_b, w_sp=w_sp, b_sp=b_sp,
             w_out_c=w_out_c, b_out_c=b_out_c,
             w_router=w_router, b_router=b_router, w_gate_e=w_gate_e, w_up_e=w_up_e, w_down_e=w_down_e,
             w_gate_s=w_gate_s, w_up_s=w_up_s, w_down_s=w_down_s)
    hist_a0 = jnp.zeros((N_EVEN, x_prompt.shape[0], CONV_A - 1, D_A), x_prompt.dtype)
    hist_b0 = jnp.zeros((N_EVEN, x_prompt.shape[0], CONV_B - 1, D_B), x_prompt.dtype)
    y_prompt, conv_a_prompt, conv_b_prompt, chunk_v_prompt = trunk(x_prompt, c_prompt, hist_a0, hist_b0, p)
    y_sample, conv_a_sample, conv_b_sample, chunk_v_sample = trunk(x_sample, c_sample, state_conv_a, state_conv_b, p)
    return (y_prompt, y_sample, conv_a_prompt, conv_b_prompt, chunk_v_prompt, conv_a_sample, conv_b_sample, chunk_v_sample)
```

```python
import functools

import jax
import jax.numpy as jnp
from jax import lax
from jax.experimental import pallas as pl
from jax.experimental.pallas import tpu as pltpu

D_MODEL = 2048
BATCH = 4
SEQ = 2048
DEPTH = 2
DEC_BATCH = 128
DEC_SEQ = 4
D_A = D_MODEL // 2
CONV_A = 3
D_B = D_MODEL // 2
CONV_B = 31
D_IN_AB = 3 * D_A + 2 * D_B
D_C = D_MODEL
H_C = 8
CHUNK = 128
N_EXPERTS = 64
TOP_K = 8
N_GROUPS = 8
TOPK_GROUPS = 4
GROUP_SIZE = N_EXPERTS // N_GROUPS
D_EXPERT = D_MODEL // 4
D_SHARED = D_MODEL // 4
ROUTED_SCALE = 2.5
ALPHA = (2 * DEPTH) ** 0.25
LN_EPS = 1e-5

T_P = BATCH * SEQ
T_S = DEC_BATCH * DEC_SEQ
T = T_P + T_S

TT = 256
N_TT = T // TT
N_TT_P = T_P // TT
TT_PER_SEQ = SEQ // TT
N_MOD = BATCH + 1

TM = 256
N_TILES = (T * TOP_K + N_EXPERTS * (TM - 1)) // TM + 1
N_PAD = N_TILES * TM
TC_DISPATCH = 512

SUBLANES = 8
C_PAD = 136

VMEM_LIMIT = 56 * 1024 * 1024


def _cp(sem):
    return pltpu.CompilerParams(dimension_semantics=sem, vmem_limit_bytes=VMEM_LIMIT)


def _sigmoid(x):
    return 1.0 / (1.0 + jnp.exp(-x))


def _silu(x):
    return x * _sigmoid(x)


def _ln(x, g, b):
    mu = jnp.mean(x, axis=-1, keepdims=True)
    xc = x - mu
    var = jnp.mean(xc * xc, axis=-1, keepdims=True)
    return xc * lax.rsqrt(var + LN_EPS) * g + b


def _bf(x):
    return x.astype(jnp.bfloat16)


def _mod_map(i):
    return (jnp.minimum(i // TT_PER_SEQ, BATCH), 0, 0)


def _ada_kernel(c_ref, w_ref, b_ref, o_ref):
    s = _bf(_silu(c_ref[...]))
    o_ref[...] = jnp.dot(s, _bf(w_ref[...]), preferred_element_type=jnp.float32) + b_ref[...]


def _ada(c_all, w_ada, b_ada):
    tn = 1024
    n = 6 * D_MODEL // tn
    return pl.pallas_call(
        _ada_kernel,
        grid=(DEPTH, n),
        in_specs=[
            pl.BlockSpec((C_PAD, D_MODEL), lambda l, j: (0, 0)),
            pl.BlockSpec((None, D_MODEL, tn), lambda l, j: (l, 0, j)),
            pl.BlockSpec((None, 1, tn), lambda l, j: (l, 0, j)),
        ],
        out_specs=pl.BlockSpec((None, C_PAD, tn), lambda l, j: (l, 0, j)),
        out_shape=jax.ShapeDtypeStruct((DEPTH, C_PAD, 6 * D_MODEL), jnp.float32),
        compiler_params=_cp(("arbitrary", "arbitrary")),
        name="ada_mod",
    )(c_all, w_ada, b_ada.reshape(DEPTH, 1, 6 * D_MODEL))


def _mod_blocks(m):
    p = jnp.broadcast_to(m[:BATCH, None, :], (BATCH, TT, D_MODEL))
    s = jnp.tile(m[BATCH:BATCH + DEC_BATCH], (TT // DEC_BATCH, 1))[None]
    return jnp.concatenate([p, s], axis=0)


def _mixab_in_kernel(x_ref, sc_ref, sh_ref, wh_ref, wb_ref, wc_ref, wl_ref, wg_ref,
                     bh_ref, bb_ref, bc_ref, bl_ref, bg_ref, p_ref, ab_ref, g_ref, wbf_ref):
    @pl.when(pl.program_id(1) == 0)
    def _():
        for i, w in enumerate((wh_ref, wb_ref, wc_ref, wl_ref, wg_ref)):
            wbf_ref[i] = _bf(w[...])

    h = _bf(x_ref[...] * (1.0 + sc_ref[...]) + sh_ref[...])

    def proj(i, b):
        return jnp.dot(h, wbf_ref[i], preferred_element_type=jnp.float32) + b[...]

    p_ref[...] = proj(2, bc_ref) * proj(0, bh_ref)
    ab_ref[...] = proj(1, bb_ref)
    g_ref[...] = proj(3, bl_ref) * _sigmoid(proj(4, bg_ref))


def _mixab_in(x_all, sc, sh, w_in, b_in):
    tn = 256
    nb = D_A // tn
    b2 = b_in.reshape(1, D_IN_AB)
    wspec = [pl.BlockSpec((D_MODEL, tn), functools.partial(lambda c, n, m: (0, c * nb + n), c)) for c in range(5)]
    bspec = [pl.BlockSpec((1, tn), functools.partial(lambda c, n, m: (0, c * nb + n), c)) for c in range(5)]
    ospec = pl.BlockSpec((TT, tn), lambda n, m: (m, n))
    oshape = jax.ShapeDtypeStruct((T, D_A), jnp.float32)
    return pl.pallas_call(
        _mixab_in_kernel,
        grid=(nb, N_TT),
        in_specs=[pl.BlockSpec((TT, D_MODEL), lambda n, m: (m, 0)),
                  pl.BlockSpec((None, TT, D_MODEL), lambda n, m: _mod_map(m)),
                  pl.BlockSpec((None, TT, D_MODEL), lambda n, m: _mod_map(m))] + wspec + bspec,
        out_specs=[ospec, ospec, ospec],
        out_shape=[oshape, oshape, oshape],
        scratch_shapes=[pltpu.VMEM((5, D_MODEL, tn), jnp.bfloat16)],
        compiler_params=_cp(("arbitrary", "arbitrary")),
        name="mixab_in",
    )(x_all, sc, sh, w_in, w_in, w_in, w_in, w_in, b2, b2, b2, b2, b2)


HIST_A_ROWS = 8
HIST_B_ROWS = 32
CONV_ROWS = 64
CONV_LANES = 256


def _mixab_time_prompt_kernel(p_ref, ab_ref, g_ref, wa_ref, wb_ref, bcb_ref, lg_ref, lb_ref,
                              o_ref, pext_ref, gext_ref, cb_ref):
    @pl.when(pl.program_id(1) == 0)
    def _():
        pext_ref[0:HIST_A_ROWS, :] = jnp.zeros((HIST_A_ROWS, D_A), jnp.float32)
        gext_ref[0:HIST_B_ROWS, :] = jnp.zeros((HIST_B_ROWS, D_B), jnp.float32)

    pext_ref[HIST_A_ROWS:HIST_A_ROWS + TT, :] = p_ref[...]
    gext_ref[HIST_B_ROWS:HIST_B_ROWS + TT, :] = g_ref[...]

    for r0 in range(0, TT, CONV_ROWS):
        for c0 in range(0, D_A, CONV_LANES):
            cs = slice(c0, c0 + CONV_LANES)
            acc = jnp.zeros((CONV_ROWS, CONV_LANES), jnp.float32)
            for k in range(CONV_A):
                off = HIST_A_ROWS - (CONV_A - 1) + k + r0
                acc = acc + wa_ref[k:k + 1, cs] * pext_ref[off:off + CONV_ROWS, cs]
            o_ref[r0:r0 + CONV_ROWS, cs] = ab_ref[r0:r0 + CONV_ROWS, cs] * acc
            acc = jnp.zeros((CONV_ROWS, CONV_LANES), jnp.float32)
            for k in range(CONV_B):
                off = HIST_B_ROWS - (CONV_B - 1) + k + r0
                acc = acc + wb_ref[k:k + 1, cs] * gext_ref[off:off + CONV_ROWS, cs]
            cb_ref[r0:r0 + CONV_ROWS, cs] = acc

    yb = _silu(_ln(cb_ref[...] + bcb_ref[...], lg_ref[...], lb_ref[...]))
    o_ref[:, D_A:] = yb

    pext_ref[0:HIST_A_ROWS, :] = pext_ref[TT:TT + HIST_A_ROWS, :]
    gext_ref[0:HIST_B_ROWS, :] = gext_ref[TT:TT + HIST_B_ROWS, :]


def _mixab_time_prompt(p, ab, g, w_conv_a, w_conv_b, b_conv_b, ln_g, ln_b):
    tspec = pl.BlockSpec((TT, D_A), lambda b, i: (b * TT_PER_SEQ + i, 0))

    def full(shape):
        return pl.BlockSpec(shape, lambda b, i: (0, 0))

    return pl.pallas_call(
        _mixab_time_prompt_kernel,
        grid=(BATCH, TT_PER_SEQ),
        in_specs=[tspec, tspec, tspec, full((CONV_A, D_A)), full((CONV_B, D_B)),
                  full((1, D_B)), full((1, D_B)), full((1, D_B))],
        out_specs=pl.BlockSpec((TT, D_MODEL), lambda b, i: (b * TT_PER_SEQ + i, 0)),
        out_shape=jax.ShapeDtypeStruct((T_P, D_MODEL), jnp.float32),
        scratch_shapes=[pltpu.VMEM((TT + HIST_A_ROWS, D_A), jnp.float32),
                        pltpu.VMEM((TT + HIST_B_ROWS, D_B), jnp.float32),
                        pltpu.VMEM((TT, D_B), jnp.float32)],
        compiler_params=_cp(("arbitrary", "arbitrary")),
        name="mixab_time_prompt",
    )(p, ab, g, w_conv_a, w_conv_b, b_conv_b.reshape(1, D_B), ln_g.reshape(1, D_B), ln_b.reshape(1, D_B))


def _mixab_time_sample_kernel(p_ref, ab_ref, g_ref, ha_ref, hb_ref, wa_ref, wb_ref, bcb_ref, lg_ref, lb_ref,
                              o_ref, cb_ref):
    def full_a(j, cs):
        n = CONV_A - 1
        return ha_ref[j, :, cs] if j < n else p_ref[j - n, :, cs]

    def full_b(j, cs):
        n = CONV_B - 1
        return hb_ref[j, :, cs] if j < n else g_ref[j - n, :, cs]

    for t in range(DEC_SEQ):
        for c0 in range(0, D_A, CONV_LANES):
            cs = slice(c0, c0 + CONV_LANES)
            acc = jnp.zeros((DEC_BATCH, CONV_LANES), jnp.float32)
            for k in range(CONV_A):
                acc = acc + wa_ref[k:k + 1, cs] * full_a(t + k, cs)
            o_ref[t, :, cs] = ab_ref[t, :, cs] * acc
            acc = jnp.zeros((DEC_BATCH, CONV_LANES), jnp.float32)
            for k in range(CONV_B):
                acc = acc + wb_ref[k:k + 1, cs] * full_b(t + k, cs)
            cb_ref[t, :, cs] = acc
    for t in range(DEC_SEQ):
        o_ref[t, :, D_A:] = _silu(_ln(cb_ref[t] + bcb_ref[...], lg_ref[...], lb_ref[...]))


def _mixab_time_sample(p_s, ab_s, g_s, hist_a, hist_b, w_conv_a, w_conv_b, b_conv_b, ln_g, ln_b):
    def full(shape):
        return pl.BlockSpec(shape, lambda i: (0,) * len(shape))

    tshape = (DEC_SEQ, DEC_BATCH, D_A)
    return pl.pallas_call(
        _mixab_time_sample_kernel,
        grid=(1,),
        in_specs=[full(tshape), full(tshape), full(tshape),
                  full((CONV_A - 1, DEC_BATCH, D_A)), full((CONV_B - 1, DEC_BATCH, D_B)),
                  full((CONV_A, D_A)), full((CONV_B, D_B)), full((1, D_B)), full((1, D_B)), full((1, D_B))],
        out_specs=full((DEC_SEQ, DEC_BATCH, D_MODEL)),
        out_shape=jax.ShapeDtypeStruct((DEC_SEQ, DEC_BATCH, D_MODEL), jnp.float32),
        scratch_shapes=[pltpu.VMEM((DEC_SEQ, DEC_BATCH, D_B), jnp.float32)],
        compiler_params=_cp(("arbitrary",)),
        name="mixab_time_sample",
    )(p_s, ab_s, g_s, hist_a, hist_b, w_conv_a, w_conv_b,
      b_conv_b.reshape(1, D_B), ln_g.reshape(1, D_B), ln_b.reshape(1, D_B))


def _out_ln_kernel(y_ref, w_ref, b_ref, x_ref, gate_ref, lg_ref, lb_ref, o_ref):
    o = jnp.dot(_bf(y_ref[...]), w_ref[...], preferred_element_type=jnp.float32) + b_ref[...]
    r = ALPHA * x_ref[...] + (1.0 + gate_ref[...]) * o
    o_ref[...] = _ln(r, lg_ref[...], lb_ref[...])


def _out_ln(y, w_bf, b, x_all, gate, ln_g, ln_b):
    tile = pl.BlockSpec((TT, D_MODEL), lambda i: (i, 0))
    row = pl.BlockSpec((1, D_MODEL), lambda i: (0, 0))
    return pl.pallas_call(
        _out_ln_kernel,
        grid=(N_TT,),
        in_specs=[tile, pl.BlockSpec((D_MODEL, D_MODEL), lambda i: (0, 0)), row, tile,
                  pl.BlockSpec((None, TT, D_MODEL), _mod_map), row, row],
        out_specs=tile,
        out_shape=jax.ShapeDtypeStruct((T, D_MODEL), jnp.float32),
        compiler_params=_cp(("arbitrary",)),
        name="out_ln",
    )(y, w_bf, b.reshape(1, D_MODEL), x_all, gate, ln_g.reshape(1, D_MODEL), ln_b.reshape(1, D_MODEL))


def _first_index(mask, iota_f, n):
    return jnp.min(jnp.where(mask, iota_f, float(n)), axis=0, keepdims=True)


def _router_kernel(x_ref, sc_ref, sh_ref, wr_ref, br_ref,
                   hp_ref, idx_ref, wts_ref, rank_ref, cnt_ref, base_ref):
    @pl.when(pl.program_id(0) == 0)
    def _():
        base_ref[...] = jnp.zeros_like(base_ref)

    h = x_ref[...] * (1.0 + sc_ref[...]) + sh_ref[...]

    hp_ref[...] = h

    logits = lax.dot_general(wr_ref[...], h, (((1,), (1,)), ((), ())),
                             precision=lax.Precision.HIGHEST, preferred_element_type=jnp.float32)
    scores = _sigmoid(logits)
    choice = scores + br_ref[...]
    neg = -jnp.inf

    iota_g = lax.broadcasted_iota(jnp.int32, (GROUP_SIZE, TT), 0).astype(jnp.float32)
    iota_n = lax.broadcasted_iota(jnp.int32, (N_GROUPS, TT), 0).astype(jnp.float32)
    gs = jnp.zeros((N_GROUPS, TT), jnp.float32)
    for g in range(N_GROUPS):
        blk = choice[g * GROUP_SIZE:(g + 1) * GROUP_SIZE, :]
        m1 = jnp.max(blk, axis=0, keepdims=True)
        first = _first_index(blk == m1, iota_g, GROUP_SIZE)
        m2 = jnp.max(jnp.where(iota_g == first, neg, blk), axis=0, keepdims=True)
        gs = jnp.where(iota_n == float(g), m1 + m2, gs)

    gsel = jnp.zeros((N_GROUPS, TT), jnp.float32)
    for _ in range(TOPK_GROUPS):
        m = jnp.max(gs, axis=0, keepdims=True)
        first = _first_index(gs == m, iota_n, N_GROUPS)
        hit = iota_n == first
        gsel = jnp.where(hit, 1.0, gsel)
        gs = jnp.where(hit, neg, gs)

    masked = jnp.concatenate(
        [jnp.where(gsel[g:g + 1, :] > 0.5, choice[g * GROUP_SIZE:(g + 1) * GROUP_SIZE, :], neg)
         for g in range(N_GROUPS)], axis=0)

    iota_e = lax.broadcasted_iota(jnp.int32, (N_EXPERTS, TT), 0).astype(jnp.float32)
    iota_k = lax.broadcasted_iota(jnp.int32, (TOP_K, TT), 0)
    idx = jnp.zeros((TOP_K, TT), jnp.float32)
    wts = jnp.zeros((TOP_K, TT), jnp.float32)
    chosen = jnp.zeros((N_EXPERTS, TT), jnp.float32)
    for k in range(TOP_K):
        m = jnp.max(masked, axis=0, keepdims=True)
        first = _first_index(masked == m, iota_e, N_EXPERTS)
        hit = iota_e == first
        wk = jnp.sum(jnp.where(hit, scores, 0.0), axis=0, keepdims=True)
        idx = jnp.where(iota_k == k, first, idx)
        wts = jnp.where(iota_k == k, wk, wts)
        chosen = jnp.where(hit, 1.0, chosen)
        masked = jnp.where(hit, neg, masked)

    wts_ref[...] = wts / jnp.sum(wts, axis=0, keepdims=True) * ROUTED_SCALE
    idx_ref[...] = idx.astype(jnp.int32)

    r_io = lax.broadcasted_iota(jnp.int32, (TT, TT), 0)
    c_io = lax.broadcasted_iota(jnp.int32, (TT, TT), 1)
    before = jnp.where(r_io < c_io, 1.0, 0.0).astype(jnp.bfloat16)
    tot = base_ref[...] + jnp.dot(_bf(chosen), before, preferred_element_type=jnp.float32)
    rank = jnp.zeros((TOP_K, TT), jnp.float32)
    for k in range(TOP_K):
        rk = jnp.sum(jnp.where(iota_e == idx[k:k + 1, :], tot, 0.0), axis=0, keepdims=True)
        rank = jnp.where(iota_k == k, rk, rank)
    rank_ref[...] = rank.astype(jnp.int32)

    base_ref[...] = base_ref[...] + jnp.sum(chosen, axis=1, keepdims=True)
    cnt_ref[...] = base_ref[...].astype(jnp.int32)


def _router(x_all, sc, sh, wr_t, b_router):
    tile = pl.BlockSpec((TT, D_MODEL), lambda i: (i, 0))
    mod = pl.BlockSpec((None, TT, D_MODEL), _mod_map)
    kt = pl.BlockSpec((TOP_K, TT), lambda i: (0, i))
    return pl.pallas_call(
        _router_kernel,
        grid=(N_TT,),
        in_specs=[tile, mod, mod,
                  pl.BlockSpec((N_EXPERTS, D_MODEL), lambda i: (0, 0)),
                  pl.BlockSpec((N_EXPERTS, 1), lambda i: (0, 0))],
        out_specs=[tile, kt, kt, kt,
                   pl.BlockSpec((N_EXPERTS, 1), lambda i: (0, 0))],
        out_shape=[jax.ShapeDtypeStruct((T, D_MODEL), jnp.float32),
                   jax.ShapeDtypeStruct((TOP_K, T), jnp.int32),
                   jax.ShapeDtypeStruct((TOP_K, T), jnp.float32),
                   jax.ShapeDtypeStruct((TOP_K, T), jnp.int32),
                   jax.ShapeDtypeStruct((N_EXPERTS, 1), jnp.int32)],
        scratch_shapes=[pltpu.VMEM((N_EXPERTS, 1), jnp.float32)],
        compiler_params=_cp(("arbitrary",)),
        name="moe_router",
    )(x_all, sc, sh, wr_t, b_router.reshape(N_EXPERTS, 1))


def _dispatch_kernel(pad_lo_ref, pad_hi_ref, nv_ref, pos_ref, hp_ref, xs_ref, sem):
    step = pl.program_id(0)
    t0 = step * TC_DISPATCH

    def row_copy(src_row, dst_row):
        return pltpu.make_async_copy(hp_ref.at[pl.ds(src_row, 1)], xs_ref.at[pl.ds(dst_row, 1)], sem)

    def wait_rows(n):
        pltpu.make_async_copy(hp_ref.at[pl.ds(0, n)], xs_ref.at[pl.ds(0, n)], sem).wait()

    def issue(t, carry):
        for k in range(TOP_K):
            row_copy(t0 + t, pos_ref[k, t]).start()
        return carry

    lax.fori_loop(0, TC_DISPATCH, issue, 0)
    for _ in range(TOP_K):
        wait_rows(TC_DISPATCH)

    @pl.when(step == 0)
    def _():
        def per_expert(e, carry):
            lo = pad_lo_ref[e]
            hi = pad_hi_ref[e]
            mid = jnp.minimum(lax.shift_left(lax.shift_right_logical(lo + (SUBLANES - 1), 3), 3), hi)

            def single(r, c):
                row_copy(0, r).start()
                return c

            def single_wait(r, c):
                row_copy(0, r).wait()
                return c

            lax.fori_loop(lo, mid, single, 0)
            lax.fori_loop(lo, mid, single_wait, 0)

            nblk = lax.shift_right_logical(hi - mid, 3)

            def block(i, c):
                dst = pl.multiple_of(mid + i * SUBLANES, SUBLANES)
                pltpu.make_async_copy(hp_ref.at[pl.ds(0, SUBLANES)], xs_ref.at[pl.ds(dst, SUBLANES)], sem).start()
                return c

            lax.fori_loop(0, nblk, block, 0)

            @pl.when(nblk > 0)
            def _():
                wait_rows(pl.multiple_of(nblk * SUBLANES, SUBLANES))

            return carry

        lax.fori_loop(0, N_EXPERTS, per_expert, 0)

        def tail_copy(j):
            dst = pl.multiple_of(j * TM, TM)
            return pltpu.make_async_copy(hp_ref.at[pl.ds(0, TM)], xs_ref.at[pl.ds(dst, TM)], sem)

        def tail_start(j, carry):
            tail_copy(j).start()
            return carry

        def tail_wait(j, carry):
            tail_copy(j).wait()
            return carry

        lax.fori_loop(nv_ref[0], N_TILES, tail_start, 0)
        lax.fori_loop(nv_ref[0], N_TILES, tail_wait, 0)


def _dispatch(pad_lo, pad_hi, n_valid, pos, hp):
    grid_spec = pltpu.PrefetchScalarGridSpec(
        num_scalar_prefetch=3,
        grid=(T // TC_DISPATCH,),
        in_specs=[pl.BlockSpec((TOP_K, TC_DISPATCH), lambda i, lo, hi, nv: (0, i), memory_space=pltpu.SMEM),
                  pl.BlockSpec(memory_space=pl.ANY)],
        out_specs=pl.BlockSpec(memory_space=pl.ANY),
        scratch_shapes=[pltpu.SemaphoreType.DMA],
    )
    return pl.pallas_call(
        _dispatch_kernel,
        grid_spec=grid_spec,
        out_shape=jax.ShapeDtypeStruct((N_PAD, D_MODEL), jnp.float32),
        compiler_params=_cp(("arbitrary",)),
        name="moe_dispatch",
    )(pad_lo, pad_hi, n_valid, pos, hp)


def _experts_kernel(te_ref, ts_ref, nv_ref, xs_ref, wg_ref, wu_ref, wd_ref, ys_ref, wgb_ref, wub_ref, wdb_ref):
    j = pl.program_id(0)
    prev = te_ref[jnp.maximum(j - 1, 0)]

    @pl.when((j == 0) | (te_ref[j] != prev))
    def _():
        wgb_ref[...] = _bf(wg_ref[...])
        wub_ref[...] = _bf(wu_ref[...])
        wdb_ref[...] = _bf(wd_ref[...])

    @pl.when(j < nv_ref[0])
    def _():
        x = _bf(xs_ref[...])
        g = jnp.dot(x, wgb_ref[...], preferred_element_type=jnp.float32)
        u = jnp.dot(x, wub_ref[...], preferred_element_type=jnp.float32)
        a = _bf(_silu(g) * u)
        ys_ref[...] = jnp.dot(a, wdb_ref[...], preferred_element_type=jnp.float32)

    @pl.when(j >= nv_ref[0])
    def _():
        ys_ref[...] = jnp.zeros_like(ys_ref)


def _experts(tile_expert, tile_src, n_valid, xs, w_gate, w_up, w_down):
    grid_spec = pltpu.PrefetchScalarGridSpec(
        num_scalar_prefetch=3,
        grid=(N_TILES,),
        in_specs=[pl.BlockSpec((TM, D_MODEL), lambda j, te, ts, nv: (ts[j], 0)),
                  pl.BlockSpec((None, D_MODEL, D_EXPERT), lambda j, te, ts, nv: (te[j], 0, 0)),
                  pl.BlockSpec((None, D_MODEL, D_EXPERT), lambda j, te, ts, nv: (te[j], 0, 0)),
                  pl.BlockSpec((None, D_EXPERT, D_MODEL), lambda j, te, ts, nv: (te[j], 0, 0))],
        out_specs=pl.BlockSpec((TM, D_MODEL), lambda j, te, ts, nv: (j, 0)),
        scratch_shapes=[pltpu.VMEM((D_MODEL, D_EXPERT), jnp.bfloat16),
                        pltpu.VMEM((D_MODEL, D_EXPERT), jnp.bfloat16),
                        pltpu.VMEM((D_EXPERT, D_MODEL), jnp.bfloat16)],
    )
    return pl.pallas_call(
        _experts_kernel,
        grid_spec=grid_spec,
        out_shape=jax.ShapeDtypeStruct((N_PAD, D_MODEL), jnp.float32),
        compiler_params=_cp(("arbitrary",)),
        name="moe_experts",
    )(tile_expert, tile_src, n_valid, xs, w_gate, w_up, w_down)


def _combine_kernel(pos_ref, ys_ref, wt_ref, x_ref, sc_ref, sh_ref, gate_ref,
                    wgs_ref, wus_ref, wds_ref, lg_ref, lb_ref, o_ref, buf_ref, sem):
    def issue(t, carry):
        for k in range(TOP_K):
            pltpu.make_async_copy(ys_ref.at[pl.ds(pos_ref[k, t], 1)], buf_ref.at[k, pl.ds(t, 1)], sem).start()
        return carry

    lax.fori_loop(0, TT, issue, 0)

    x = x_ref[...]
    h = _bf(x * (1.0 + sc_ref[...]) + sh_ref[...])
    sg = jnp.dot(h, wgs_ref[...], preferred_element_type=jnp.float32)
    su = jnp.dot(h, wus_ref[...], preferred_element_type=jnp.float32)
    f = jnp.dot(_bf(_silu(sg) * su), wds_ref[...], preferred_element_type=jnp.float32)

    for k in range(TOP_K):
        pltpu.make_async_copy(ys_ref.at[pl.ds(0, TT)], buf_ref.at[k], sem).wait()
    wt = wt_ref[...]
    for k in range(TOP_K):
        f = f + wt[:, k:k + 1] * buf_ref[k]

    r = ALPHA * x + (1.0 + gate_ref[...]) * f
    o_ref[...] = _ln(r, lg_ref[...], lb_ref[...])


def _combine(pos, ys, wts_t, x_all, sc, sh, gate, wgs_bf, wus_bf, wds_bf, ln_g, ln_b):
    tile = pl.BlockSpec((TT, D_MODEL), lambda i: (i, 0))
    mod = pl.BlockSpec((None, TT, D_MODEL), _mod_map)
    row = pl.BlockSpec((1, D_MODEL), lambda i: (0, 0))
    return pl.pallas_call(
        _combine_kernel,
        grid=(N_TT,),
        in_specs=[pl.BlockSpec((TOP_K, TT), lambda i: (0, i), memory_space=pltpu.SMEM),
                  pl.BlockSpec(memory_space=pl.ANY),
                  pl.BlockSpec((TT, TOP_K), lambda i: (i, 0)),
                  tile, mod, mod, mod,
                  pl.BlockSpec((D_MODEL, D_SHARED), lambda i: (0, 0)),
                  pl.BlockSpec((D_MODEL, D_SHARED), lambda i: (0, 0)),
                  pl.BlockSpec((D_SHARED, D_MODEL), lambda i: (0, 0)),
                  row, row],
        out_specs=tile,
        out_shape=jax.ShapeDtypeStruct((T, D_MODEL), jnp.float32),
        scratch_shapes=[pltpu.VMEM((TOP_K, TT, D_MODEL), jnp.float32), pltpu.SemaphoreType.DMA],
        compiler_params=_cp(("arbitrary",)),
        name="moe_combine",
    )(pos, ys, wts_t, x_all, sc, sh, gate, wgs_bf, wus_bf, wds_bf,
      ln_g.reshape(1, D_MODEL), ln_b.reshape(1, D_MODEL))


def _moe(x_all, sc, sh, gate, w_router, b_router, w_gate_e, w_up_e, w_down_e,
         w_gate_s, w_up_s, w_down_s, ln_g, ln_b):
    hp, idx, wts, rank, counts = _router(x_all, sc, sh, w_router.T, b_router)

    counts = counts.reshape(N_EXPERTS)
    tiles_e = (counts + (TM - 1)) // TM
    tile_end = jnp.cumsum(tiles_e)
    row_off = (tile_end - tiles_e) * TM
    n_valid = tile_end[-1]
    jt = jnp.arange(N_TILES, dtype=jnp.int32)
    tile_src = jnp.minimum(jt, n_valid - 1)
    tile_expert = jnp.minimum(jnp.sum((tile_end[None, :] <= tile_src[:, None]).astype(jnp.int32), axis=1),
                              N_EXPERTS - 1)
    pos = (jnp.take(row_off, idx) + rank).astype(jnp.int32)
    pad_lo = (row_off + counts).astype(jnp.int32)
    pad_hi = (row_off + tiles_e * TM).astype(jnp.int32)

    n_valid = n_valid.reshape(1).astype(jnp.int32)
    xs = _dispatch(pad_lo, pad_hi, n_valid, pos, hp)
    ys = _experts(tile_expert, tile_src.astype(jnp.int32), n_valid, xs, w_gate_e, w_up_e, w_down_e)
    return _combine(pos, ys, wts.T, x_all, sc, sh, gate, _bf(w_gate_s), _bf(w_up_s), _bf(w_down_s), ln_g, ln_b)


def _mixc_in_kernel(x_ref, sc_ref, sh_ref, w_ref, b_ref, o_ref, wbf_ref):
    @pl.when(pl.program_id(1) == 0)
    def _():
        wbf_ref[...] = _bf(w_ref[...])

    h = _bf(x_ref[...] * (1.0 + sc_ref[...]) + sh_ref[...])
    z = jnp.dot(h, wbf_ref[...], preferred_element_type=jnp.float32) + b_ref[...]
    o_ref[...] = jax.nn.gelu(z, approximate=True)


def _mixc_in(x_all, sc, sh, w_in, b_in):
    tn = 1024
    return pl.pallas_call(
        _mixc_in_kernel,
        grid=(2 * D_C // tn, N_TT),
        in_specs=[pl.BlockSpec((TT, D_MODEL), lambda n, m: (m, 0)),
                  pl.BlockSpec((None, TT, D_MODEL), lambda n, m: _mod_map(m)),
                  pl.BlockSpec((None, TT, D_MODEL), lambda n, m: _mod_map(m)),
                  pl.BlockSpec((D_MODEL, tn), lambda n, m: (0, n)),
                  pl.BlockSpec((1, tn), lambda n, m: (0, n))],
        out_specs=pl.BlockSpec((TT, tn), lambda n, m: (m, n)),
        out_shape=jax.ShapeDtypeStruct((T, 2 * D_C), jnp.float32),
        scratch_shapes=[pltpu.VMEM((D_MODEL, tn), jnp.bfloat16)],
        compiler_params=_cp(("arbitrary", "arbitrary")),
        name="mixc_in",
    )(x_all, sc, sh, w_in, b_in.reshape(1, 2 * D_C))


HEAD_W = D_C // H_C
CHUNKS_PER_SEQ = SEQ // CHUNK


def _chunkmix_prompt_kernel(u_ref, v_ref, lg_ref, lb_ref, w_ref, bias_ref, y_ref, vs_ref):
    vn = _ln(v_ref[...], lg_ref[...], lb_ref[...])
    vs_ref[...] = vn
    vb = _bf(vn)
    r_io = lax.broadcasted_iota(jnp.int32, (CHUNK, CHUNK), 0)
    c_io = lax.broadcasted_iota(jnp.int32, (CHUNK, CHUNK), 1)
    for hd in range(H_C):
        cs = slice(hd * HEAD_W, (hd + 1) * HEAD_W)
        w = _bf(jnp.where(c_io <= r_io, w_ref[hd], 0.0))
        mix = jnp.dot(w, vb[:, cs], preferred_element_type=jnp.float32) + bias_ref[:, cs]
        y_ref[:, cs] = u_ref[:, cs] * mix


def _chunkmix_prompt(zc, ln_g, ln_b, w_sp, bias_full):
    row = pl.BlockSpec((1, D_C), lambda c: (0, 0))
    return pl.pallas_call(
        _chunkmix_prompt_kernel,
        grid=(T_P // CHUNK,),
        in_specs=[pl.BlockSpec((CHUNK, D_C), lambda c: (c, 0)),
                  pl.BlockSpec((CHUNK, D_C), lambda c: (c, 1)),
                  row, row,
                  pl.BlockSpec((H_C, CHUNK, CHUNK), lambda c: (0, 0, 0)),
                  pl.BlockSpec((CHUNK, D_C), lambda c: (0, 0))],
        out_specs=[pl.BlockSpec((CHUNK, D_C), lambda c: (c, 0)),
                   pl.BlockSpec((None, CHUNK, D_C), lambda c: (c // CHUNKS_PER_SEQ, 0, 0))],
        out_shape=[jax.ShapeDtypeStruct((T_P, D_C), jnp.float32),
                   jax.ShapeDtypeStruct((BATCH, CHUNK, D_C), jnp.float32)],
        compiler_params=_cp(("arbitrary",)),
        name="chunkmix_prompt",
    )(zc, zc, ln_g.reshape(1, D_C), ln_b.reshape(1, D_C), w_sp, bias_full)


def _chunkmix_sample_kernel(u_ref, v_ref, lg_ref, lb_ref, wcol_ref, bias_ref, y_ref, vs_ref):
    for t in range(DEC_SEQ):
        vs_ref[t] = _ln(v_ref[t], lg_ref[...], lb_ref[...])
    for t in range(DEC_SEQ):
        mix = bias_ref[t:t + 1, :]
        for j in range(t + 1):
            mix = mix + wcol_ref[t, j:j + 1, :] * vs_ref[j]
        y_ref[t] = u_ref[t] * mix


def _chunkmix_sample(zc_s, ln_g, ln_b, wcol, bias_s):
    def full(shape):
        return pl.BlockSpec(shape, lambda i: (0,) * len(shape))

    tile = (DEC_SEQ, DEC_BATCH, D_C)
    return pl.pallas_call(
        _chunkmix_sample_kernel,
        grid=(1,),
        in_specs=[pl.BlockSpec(tile, lambda i: (0, 0, 0)),
                  pl.BlockSpec(tile, lambda i: (0, 0, 1)),
                  full((1, D_C)), full((1, D_C)), full((DEC_SEQ, DEC_SEQ, D_C)), full((DEC_SEQ, D_C))],
        out_specs=[full(tile), full(tile)],
        out_shape=[jax.ShapeDtypeStruct(tile, jnp.float32), jax.ShapeDtypeStruct(tile, jnp.float32)],
        compiler_params=_cp(("arbitrary",)),
        name="chunkmix_sample",
    )(zc_s, zc_s, ln_g.reshape(1, D_C), ln_b.reshape(1, D_C), wcol, bias_s)


def kernel(x_prompt, x_sample, state_conv_a, state_conv_b, c_prompt, c_sample, w_ada, b_ada, ln_g, ln_b, w_in_ab, b_in_ab, w_conv_a, w_conv_b, b_conv_b, ln_cb_g, ln_cb_b, w_out_ab, b_out_ab, w_in_c, b_in_c, ln_v_g, ln_v_b, w_sp, b_sp, w_out_c, b_out_c, w_router, b_router, w_gate_e, w_up_e, w_down_e, w_gate_s, w_up_s, w_down_s):
    f32 = jnp.float32
    x_all = jnp.concatenate([x_prompt.reshape(T_P, D_MODEL),
                             jnp.transpose(x_sample, (1, 0, 2)).reshape(T_S, D_MODEL)], axis=0)
    c_all = jnp.concatenate([c_prompt, c_sample, jnp.zeros((C_PAD - BATCH - DEC_BATCH, D_MODEL), f32)], axis=0)
    mods = _ada(c_all, w_ada, b_ada)

    def mod(l, j):
        return _mod_blocks(mods[l, :, j * D_MODEL:(j + 1) * D_MODEL])

    def moe(x, l):
        return _moe(x, mod(l, 4), mod(l, 3), mod(l, 5), w_router[l], b_router[l],
                    w_gate_e[l], w_up_e[l], w_down_e[l], w_gate_s[l], w_up_s[l], w_down_s[l],
                    ln_g[l, 1], ln_b[l, 1])

    p, ab, g = _mixab_in(x_all, mod(0, 1), mod(0, 0), w_in_ab[0], b_in_ab[0])
    yab_p = _mixab_time_prompt(p, ab, g, w_conv_a[0], w_conv_b[0], b_conv_b[0], ln_cb_g[0], ln_cb_b[0])
    hist_a = jnp.transpose(state_conv_a[0], (1, 0, 2))
    hist_b = jnp.transpose(state_conv_b[0], (1, 0, 2))
    p_s = p[T_P:].reshape(DEC_SEQ, DEC_BATCH, D_A)
    g_s = g[T_P:].reshape(DEC_SEQ, DEC_BATCH, D_B)
    yab_s = _mixab_time_sample(p_s, ab[T_P:].reshape(DEC_SEQ, DEC_BATCH, D_A), g_s, hist_a, hist_b,
                               w_conv_a[0], w_conv_b[0], b_conv_b[0], ln_cb_g[0], ln_cb_b[0])
    yab = jnp.concatenate([yab_p, yab_s.reshape(T_S, D_MODEL)], axis=0)
    x1 = _out_ln(yab, _bf(w_out_ab[0]), b_out_ab[0], x_all, mod(0, 2), ln_g[0, 0], ln_b[0, 0])
    x2 = moe(x1, 0)

    p3 = p[:T_P].reshape(BATCH, SEQ, D_A)
    g3 = g[:T_P].reshape(BATCH, SEQ, D_B)
    conv_a_prompt = p3[:, SEQ - (CONV_A - 1):][None]
    conv_b_prompt = g3[:, SEQ - (CONV_B - 1):][None]
    conv_a_sample = jnp.transpose(p_s[DEC_SEQ - (CONV_A - 1):], (1, 0, 2))[None]
    conv_b_sample = jnp.transpose(jnp.concatenate([hist_b[DEC_SEQ:], g_s], axis=0), (1, 0, 2))[None]

    zc = _mixc_in(x2, mod(1, 1), mod(1, 0), w_in_c[0], b_in_c[0])
    bias_full = jnp.repeat(jnp.transpose(b_sp[0]), HEAD_W, axis=1)
    y_p, vstate_p = _chunkmix_prompt(zc, ln_v_g[0], ln_v_b[0], w_sp[0], bias_full)
    wcol = jnp.repeat(jnp.transpose(w_sp[0, :, :DEC_SEQ, :DEC_SEQ], (1, 2, 0)), HEAD_W, axis=2)
    y_s, vstate_s = _chunkmix_sample(zc[T_P:].reshape(DEC_SEQ, DEC_BATCH, 2 * D_C), ln_v_g[0], ln_v_b[0],
                                     wcol, bias_full[:DEC_SEQ])
    y2 = jnp.concatenate([y_p, y_s.reshape(T_S, D_C)], axis=0)
    x3 = _out_ln(y2, _bf(w_out_c[0]), b_out_c[0], x2, mod(1, 2), ln_g[1, 0], ln_b[1, 0])
    x4 = moe(x3, 1)

    y_prompt = x4[:T_P].reshape(BATCH, SEQ, D_MODEL)
    y_sample = jnp.transpose(x4[T_P:].reshape(DEC_SEQ, DEC_BATCH, D_MODEL), (1, 0, 2))
    chunk_v_prompt = vstate_p[None]
    chunk_v_sample = jnp.transpose(vstate_s, (1, 0, 2))[None]
    return (y_prompt, y_sample, conv_a_prompt, conv_b_prompt, chunk_v_prompt,
            conv_a_sample, conv_b_sample, chunk_v_sample)
```
